```python
import math
import jax, jax.numpy as jnp
from jax import lax
import numpy as np

D_MODEL = 1024
BATCH = 8
SEQ = 2048
DEPTH = 1

N_META = 16
D_MIX = D_MODEL
D_CONV = D_MIX // 2
CONV_HEAD_DIM = 64
N_CONV_HEADS = D_CONV // CONV_HEAD_DIM
D_SSM = D_MIX - D_CONV
SSM_GROUP = 16
N_SSM_GROUPS = D_SSM // SSM_GROUP
SSM_STATE = 64
CONV_WIDTH = 3
D_FF = 2816
D_IN_PROJ = 3 * D_CONV + D_SSM
RMS_EPS = 1e-6
DT_MIN = 1e-3
DT_MAX = 1e-1

kernel_name = "hymba_conv_s5_hybrid_layer"


def rms_norm(x, g):
    xf = x.astype(jnp.float32)
    y = xf * lax.rsqrt(jnp.mean(xf * xf, axis=-1, keepdims=True) + RMS_EPS)
    return (y * g.astype(jnp.float32)).astype(x.dtype)


def causal_dwconv(x, w, b=None):
    c = x.shape[-1]
    y = lax.conv_general_dilated(
        x, w[:, None, :].astype(x.dtype), window_strides=(1,),
        padding=[(CONV_WIDTH - 1, 0)], dimension_numbers=("NWC", "WIO", "NWC"),
        feature_group_count=c)
    if b is not None:
        y = y + b.astype(x.dtype)
    return y


def s5_group_ssm(u, lam_re, lam_im, log_dt, b_re, b_im, c_re, c_im, d_skip, w_glu):
    bsz, seq_len, _ = u.shape
    f32 = jnp.float32
    uf = u.astype(f32).reshape(bsz, seq_len, N_SSM_GROUPS, SSM_GROUP)
    lr = lam_re.astype(f32)
    li = lam_im.astype(f32)
    dt = jnp.exp(log_dt.astype(f32))[:, None]
    mag = jnp.exp(lr * dt)
    ang = li * dt
    a_re = mag * jnp.cos(ang)
    a_im = mag * jnp.sin(ang)
    den = lr * lr + li * li
    nr = a_re - 1.0
    f_re = (nr * lr + a_im * li) / den
    f_im = (a_im * lr - nr * li) / den
    br = b_re.astype(f32)
    bi = b_im.astype(f32)
    bb_re = f_re[..., None] * br - f_im[..., None] * bi
    bb_im = f_re[..., None] * bi + f_im[..., None] * br
    bu_re = jnp.einsum("blgh,gph->blgp", uf, bb_re)
    bu_im = jnp.einsum("blgh,gph->blgp", uf, bb_im)
    a_re_t = jnp.broadcast_to(a_re[None, None], (1, seq_len, N_SSM_GROUPS, SSM_STATE))
    a_im_t = jnp.broadcast_to(a_im[None, None], (1, seq_len, N_SSM_GROUPS, SSM_STATE))

    def combine(e1, e2):
        ar1, ai1, sr1, si1 = e1
        ar2, ai2, sr2, si2 = e2
        return (ar1 * ar2 - ai1 * ai2,
                ar1 * ai2 + ai1 * ar2,
                ar2 * sr1 - ai2 * si1 + sr2,
                ar2 * si1 + ai2 * sr1 + si2)

    _, _, s_re, s_im = lax.associative_scan(combine, (a_re_t, a_im_t, bu_re, bu_im), axis=1)
    y = (jnp.einsum("blgp,ghp->blgh", s_re, c_re.astype(f32))
         - jnp.einsum("blgp,ghp->blgh", s_im, c_im.astype(f32))
         + d_skip.astype(f32) * uf)
    y = y.reshape(bsz, seq_len, D_SSM)
    g = jax.nn.gelu(y)
    out = g * jax.nn.sigmoid(g @ w_glu.astype(f32))
    return out.astype(u.dtype)


def setup_inputs(seed: int = 0) -> dict:
    key = jax.random.key(seed)
    ks = jax.random.split(key, 24)
    f32 = jnp.float32
    nrm = lambda k, shape, s: jax.random.normal(k, shape, f32) * s
    x = jax.random.normal(ks[0], (BATCH, SEQ, D_MODEL), f32)
    meta_tokens = nrm(ks[1], (N_META, D_MODEL), 1.0)
    norm_mix_g = 1.0 + nrm(ks[2], (DEPTH, D_MODEL), 0.02)
    w_in = nrm(ks[3], (DEPTH, D_MODEL, D_IN_PROJ), D_MODEL ** -0.5)
    conv_w = nrm(ks[4], (DEPTH, CONV_WIDTH, D_CONV), CONV_WIDTH ** -0.5)
    n = jnp.arange(SSM_STATE, dtype=f32)
    ssm_lam_re = -0.5 + nrm(ks[5], (DEPTH, N_SSM_GROUPS, SSM_STATE), 1e-3)
    ssm_lam_im = math.pi * n + nrm(ks[6], (DEPTH, N_SSM_GROUPS, SSM_STATE), 1e-3)
    ssm_log_dt = jax.random.uniform(ks[7], (DEPTH, N_SSM_GROUPS), f32,
                                    math.log(DT_MIN), math.log(DT_MAX))
    b_scale = (2.0 * SSM_GROUP) ** -0.5
    ssm_b_re = nrm(ks[8], (DEPTH, N_SSM_GROUPS, SSM_STATE, SSM_GROUP), b_scale)
    ssm_b_im = nrm(ks[9], (DEPTH, N_SSM_GROUPS, SSM_STATE, SSM_GROUP), b_scale)
    c_scale = (2.0 * SSM_STATE) ** -0.5
    ssm_c_re = nrm(ks[10], (DEPTH, N_SSM_GROUPS, SSM_GROUP, SSM_STATE), c_scale)
    ssm_c_im = nrm(ks[11], (DEPTH, N_SSM_GROUPS, SSM_GROUP, SSM_STATE), c_scale)
    ssm_d = nrm(ks[12], (DEPTH, N_SSM_GROUPS, SSM_GROUP), 1.0)
    ssm_w_glu = nrm(ks[13], (DEPTH, D_SSM, D_SSM), D_SSM ** -0.5)
    gain_conv_out = 1.0 + nrm(ks[14], (DEPTH, D_CONV), 0.02)
    gain_ssm_out = 1.0 + nrm(ks[15], (DEPTH, D_SSM), 0.02)
    w_out = nrm(ks[16], (DEPTH, D_MIX, D_MODEL), D_MIX ** -0.5)
    norm_ffn_g = 1.0 + nrm(ks[17], (DEPTH, D_MODEL), 0.02)
    w_up = nrm(ks[18], (DEPTH, D_MODEL, 2 * D_FF), D_MODEL ** -0.5)
    ffn_conv_w = nrm(ks[19], (DEPTH, CONV_WIDTH, 2 * D_FF), CONV_WIDTH ** -0.5)
    ffn_conv_b = nrm(ks[20], (DEPTH, 2 * D_FF), 0.01)
    w_down = nrm(ks[21], (DEPTH, D_FF, D_MODEL), D_FF ** -0.5)
    norm_final_g = 1.0 + nrm(ks[22], (D_MODEL,), 0.02)
    return {"x": x, "meta_tokens": meta_tokens, "norm_mix_g": norm_mix_g, "w_in": w_in,
            "conv_w": conv_w, "ssm_lam_re": ssm_lam_re, "ssm_lam_im": ssm_lam_im,
            "ssm_log_dt": ssm_log_dt, "ssm_b_re": ssm_b_re, "ssm_b_im": ssm_b_im,
            "ssm_c_re": ssm_c_re, "ssm_c_im": ssm_c_im, "ssm_d": ssm_d,
            "ssm_w_glu": ssm_w_glu, "gain_conv_out": gain_conv_out,
            "gain_ssm_out": gain_ssm_out, "w_out": w_out, "norm_ffn_g": norm_ffn_g,
            "w_up": w_up, "ffn_conv_w": ffn_conv_w, "ffn_conv_b": ffn_conv_b,
            "w_down": w_down, "norm_final_g": norm_final_g}


def reference(x, meta_tokens, norm_mix_g, w_in, conv_w, ssm_lam_re, ssm_lam_im, ssm_log_dt,
              ssm_b_re, ssm_b_im, ssm_c_re, ssm_c_im, ssm_d, ssm_w_glu, gain_conv_out,
              gain_ssm_out, w_out, norm_ffn_g, w_up, ffn_conv_w, ffn_conv_b, w_down,
              norm_final_g):
    bsz = x.shape[0]
    meta = jnp.broadcast_to(meta_tokens.astype(x.dtype)[None], (bsz, N_META, D_MODEL))
    h = jnp.concatenate([meta, x], axis=1)
    for i in range(DEPTH):
        hn = rms_norm(h, norm_mix_g[i])
        proj = hn @ w_in[i].astype(h.dtype)
        b_gate = proj[..., :D_CONV]
        c_gate = proj[..., D_CONV:2 * D_CONV]
        v = proj[..., 2 * D_CONV:3 * D_CONV]
        u = proj[..., 3 * D_CONV:]
        conv_out = b_gate * causal_dwconv(c_gate * v, conv_w[i])
        ssm_out = s5_group_ssm(u, ssm_lam_re[i], ssm_lam_im[i], ssm_log_dt[i],
                               ssm_b_re[i], ssm_b_im[i], ssm_c_re[i], ssm_c_im[i],
                               ssm_d[i], ssm_w_glu[i])
        mixed = jnp.concatenate([rms_norm(conv_out, gain_conv_out[i]),
                                 rms_norm(ssm_out, gain_ssm_out[i])], axis=-1)
        h = h + mixed @ w_out[i].astype(h.dtype)
        hn = rms_norm(h, norm_ffn_g[i])
        up = causal_dwconv(hn @ w_up[i].astype(h.dtype), ffn_conv_w[i], ffn_conv_b[i])
        a = up[..., :D_FF]
        val = up[..., D_FF:]
        h = h + (jax.nn.silu(a) * val) @ w_down[i].astype(h.dtype)
    y = rms_norm(h, norm_final_g)
    return y[:, N_META:]
```

```python
import functools
import math

import jax
import jax.numpy as jnp
from jax.experimental import pallas as pl
from jax.experimental.pallas import tpu as pltpu

D_MODEL = 1024
BATCH = 8
SEQ = 2048
N_META = 16
SEQ_TOTAL = SEQ + N_META
D_CONV = 512
D_SSM = 512
SSM_GROUP = 16
N_SSM_GROUPS = 32
SSM_STATE = 64
D_FF = 2816
D_IN_PROJ = 3 * D_CONV + D_SSM
RMS_EPS = 1e-6

LANES = 128
MXU_TILE = 256
STATE_COLS = 2 * N_SSM_GROUPS * SSM_STATE
STATE_HALF = STATE_COLS // 2
SCAN_TILES = 4

STEPS_PER_BLOCK = 48
ROWS = STEPS_PER_BLOCK * BATCH
N_BLOCKS = SEQ_TOTAL // STEPS_PER_BLOCK
HALO = 2 * BATCH
FF_CHUNK = 256
N_FF_CHUNKS = D_FF // FF_CHUNK
VMEM_LIMIT_BYTES = 56 * 1024 * 1024

assert SEQ_TOTAL % STEPS_PER_BLOCK == 0 and STEPS_PER_BLOCK % 2 == 0
assert D_FF % FF_CHUNK == 0


def _rms_norm(x, gain):
    ms = jnp.mean(x * x, axis=-1, keepdims=True)
    return x * jax.lax.rsqrt(ms + RMS_EPS) * gain


def _dot(a, b):
    return jnp.dot(a, b, preferred_element_type=jnp.float32)


def _causal_conv3(prev, cur, w_ref, rows):
    ext = jnp.concatenate([prev, cur], axis=0)
    return (w_ref[0:1, :] * ext[0:rows]
            + w_ref[1:2, :] * ext[BATCH:rows + BATCH]
            + w_ref[2:3, :] * ext[HALO:rows + HALO])


def _mixer_kernel(h_ref, g_mix_ref, w_in_ref, conv_w_ref, wb_ref, a_re_ref, a_im_ref,
                  wc_ref, d_ref, w_glu_ref, g_conv_ref, g_ssm_ref, w_out_ref,
                  out_ref, cv_carry, s_re_carry, s_im_carry, bu_scr, s_scr):
    @pl.when(pl.program_id(0) == 0)
    def _():
        cv_carry[...] = jnp.zeros_like(cv_carry)
        s_re_carry[...] = jnp.zeros_like(s_re_carry)
        s_im_carry[...] = jnp.zeros_like(s_im_carry)

    h = h_ref[...]
    hn = _rms_norm(h, g_mix_ref[...]).astype(jnp.bfloat16)
    proj = _dot(hn, w_in_ref[...])
    b_gate = proj[:, :D_CONV]
    cv = proj[:, D_CONV:2 * D_CONV] * proj[:, 2 * D_CONV:3 * D_CONV]
    u = proj[:, 3 * D_CONV:]

    conv_out = b_gate * _causal_conv3(cv_carry[...], cv, conv_w_ref, ROWS)
    cv_carry[...] = cv[ROWS - HALO:, :]

    u_bf = u.astype(jnp.bfloat16)
    for half in range(2):
        bu_scr[:, half * STATE_HALF:(half + 1) * STATE_HALF] = _dot(
            u_bf[:, half * MXU_TILE:(half + 1) * MXU_TILE], wb_ref[half])

    for blk in range(STATE_COLS // (2 * LANES * SCAN_TILES)):
        tiles = [blk * SCAN_TILES + k for k in range(SCAN_TILES)]
        a_re = [jnp.broadcast_to(a_re_ref[:, q * LANES:(q + 1) * LANES], (BATCH, LANES))
                for q in tiles]
        a_im = [jnp.broadcast_to(a_im_ref[:, q * LANES:(q + 1) * LANES], (BATCH, LANES))
                for q in tiles]
        col0 = tiles[0] * 2 * LANES
        width = SCAN_TILES * 2 * LANES

        def two_steps(j, carry, a_re=a_re, a_im=a_im, col0=col0, width=width):
            s_re, s_im = carry
            row = pl.multiple_of(j * HALO, HALO)
            bu = bu_scr[pl.ds(row, HALO), col0:col0 + width]
            outs = []
            for step in range(2):
                new_re, new_im, pieces = [], [], []
                for k in range(SCAN_TILES):
                    bu_re = bu[step * BATCH:(step + 1) * BATCH, 2 * k * LANES:(2 * k + 1) * LANES]
                    bu_im = bu[step * BATCH:(step + 1) * BATCH,
                               (2 * k + 1) * LANES:(2 * k + 2) * LANES]
                    n_re = a_re[k] * s_re[k] - a_im[k] * s_im[k] + bu_re
                    n_im = a_re[k] * s_im[k] + a_im[k] * s_re[k] + bu_im
                    new_re.append(n_re)
                    new_im.append(n_im)
                    pieces += [n_re, n_im]
                s_re, s_im = new_re, new_im
                outs.append(jnp.concatenate(pieces, axis=1))
            s_scr[pl.ds(row, HALO), col0:col0 + width] = (
                jnp.concatenate(outs, axis=0).astype(jnp.bfloat16))
            return s_re, s_im

        init = ([s_re_carry[:, q * LANES:(q + 1) * LANES] for q in tiles],
                [s_im_carry[:, q * LANES:(q + 1) * LANES] for q in tiles])
        s_re, s_im = jax.lax.fori_loop(0, STEPS_PER_BLOCK // 2, two_steps, init)
        for k, q in enumerate(tiles):
            s_re_carry[:, q * LANES:(q + 1) * LANES] = s_re[k]
            s_im_carry[:, q * LANES:(q + 1) * LANES] = s_im[k]

    y = jnp.concatenate(
        [_dot(s_scr[:, half * STATE_HALF:(half + 1) * STATE_HALF], wc_ref[half])
         for half in range(2)], axis=1) + d_ref[...] * u
    g = jax.nn.gelu(y, approximate=True)
    ssm_out = g * jax.nn.sigmoid(_dot(g.astype(jnp.bfloat16), w_glu_ref[...]))

    mixed = jnp.concatenate(
        [_rms_norm(conv_out, g_conv_ref[...]), _rms_norm(ssm_out, g_ssm_ref[...])],
        axis=1).astype(jnp.bfloat16)
    out_ref[...] = h + _dot(mixed, w_out_ref[...])


def _ffn_kernel(h_ref, g_ffn_ref, w_up_ref, cw_ref, cb_ref, w_down_ref, g_final_ref,
                out_ref, up_carry):
    @pl.when(pl.program_id(0) == 0)
    def _():
        up_carry[...] = jnp.zeros_like(up_carry)

    h = h_ref[...]
    hn = _rms_norm(h, g_ffn_ref[...]).astype(jnp.bfloat16)
    acc = h
    for c in range(N_FF_CHUNKS):
        halves = []
        for base in (0, D_FF):
            lo = base + c * FF_CHUNK
            up = _dot(hn, w_up_ref[:, lo:lo + FF_CHUNK])
            conv = (_causal_conv3(up_carry[:, lo:lo + FF_CHUNK], up,
                                  cw_ref.at[:, lo:lo + FF_CHUNK], ROWS)
                    + cb_ref[:, lo:lo + FF_CHUNK])
            up_carry[:, lo:lo + FF_CHUNK] = up[ROWS - HALO:, :]
            halves.append(conv)
        act = (jax.nn.silu(halves[0]) * halves[1]).astype(jnp.bfloat16)
        acc = acc + _dot(act, w_down_ref[c * FF_CHUNK:(c + 1) * FF_CHUNK, :])
    out_ref[...] = _rms_norm(acc, g_final_ref[...])


def _const_spec(shape):
    return pl.BlockSpec(shape, lambda i: (0,) * len(shape))


def _row_spec(cols):
    return pl.BlockSpec((ROWS, cols), lambda i: (i, 0))


def _s5_discretize(lam_re, lam_im, log_dt, b_re, b_im):
    dt = jnp.exp(log_dt)[:, None]
    mag = jnp.exp(lam_re * dt)
    ang = lam_im * dt
    a_re = mag * jnp.cos(ang)
    a_im = mag * jnp.sin(ang)
    den = lam_re * lam_re + lam_im * lam_im
    nr = a_re - 1.0
    f_re = (nr * lam_re + a_im * lam_im) / den
    f_im = (a_im * lam_re - nr * lam_im) / den
    bb_re = f_re[..., None] * b_re - f_im[..., None] * b_im
    bb_im = f_re[..., None] * b_im + f_im[..., None] * b_re
    return a_re, a_im, bb_re, bb_im


def _s5_block_weights(bb_re, bb_im, c_re, c_im):
    groups_per_half = N_SSM_GROUPS // 2
    eye = jnp.eye(groups_per_half, dtype=jnp.float32)
    wb, wc = [], []
    for half in range(2):
        sl = slice(half * groups_per_half, (half + 1) * groups_per_half)
        bb = jnp.stack([bb_re[sl], bb_im[sl]], axis=1)
        blk = jnp.einsum("gxph,gk->ghkxp", bb, eye)
        blk = blk.reshape(groups_per_half, SSM_GROUP, groups_per_half // 2, 2, 2, SSM_STATE)
        blk = blk.transpose(0, 1, 2, 4, 3, 5)
        wb.append(blk.reshape(MXU_TILE, STATE_HALF))
        cc = jnp.stack([c_re[sl], -c_im[sl]], axis=1)
        blk = jnp.einsum("gxhp,gk->kxpgh", cc, eye)
        blk = blk.reshape(groups_per_half // 2, 2, 2, SSM_STATE, groups_per_half, SSM_GROUP)
        blk = blk.transpose(0, 2, 1, 3, 4, 5)
        wc.append(blk.reshape(STATE_HALF, MXU_TILE))
    return jnp.stack(wb).astype(jnp.bfloat16), jnp.stack(wc).astype(jnp.bfloat16)


def kernel(x, meta_tokens, norm_mix_g, w_in, conv_w, ssm_lam_re, ssm_lam_im, ssm_log_dt, ssm_b_re, ssm_b_im, ssm_c_re, ssm_c_im, ssm_d, ssm_w_glu, gain_conv_out, gain_ssm_out, w_out, norm_ffn_g, w_up, ffn_conv_w, ffn_conv_b, w_down, norm_final_g):
    f32, bf16 = jnp.float32, jnp.bfloat16
    assert x.shape == (BATCH, SEQ, D_MODEL) and norm_mix_g.shape[0] == 1

    meta = jnp.broadcast_to(meta_tokens.astype(f32)[:, None, :], (N_META, BATCH, D_MODEL))
    h0 = jnp.concatenate([meta, jnp.swapaxes(x, 0, 1)], axis=0).reshape(
        SEQ_TOTAL * BATCH, D_MODEL)

    a_re, a_im, bb_re, bb_im = _s5_discretize(
        ssm_lam_re[0], ssm_lam_im[0], ssm_log_dt[0], ssm_b_re[0], ssm_b_im[0])
    wb, wc = _s5_block_weights(bb_re, bb_im, ssm_c_re[0], ssm_c_im[0])
    row = lambda v: v.reshape(1, -1).astype(f32)

    params = pltpu.CompilerParams(dimension_semantics=("arbitrary",),
                                  vmem_limit_bytes=VMEM_LIMIT_BYTES)
    n_rows = SEQ_TOTAL * BATCH

    h1 = pl.pallas_call(
        _mixer_kernel,
        out_shape=jax.ShapeDtypeStruct((n_rows, D_MODEL), f32),
        grid=(N_BLOCKS,),
        in_specs=[
            _row_spec(D_MODEL),
            _const_spec((1, D_MODEL)),
            _const_spec((D_MODEL, D_IN_PROJ)),
            _const_spec((3, D_CONV)),
            _const_spec((2, MXU_TILE, STATE_HALF)),
            _const_spec((1, STATE_HALF)),
            _const_spec((1, STATE_HALF)),
            _const_spec((2, STATE_HALF, MXU_TILE)),
            _const_spec((1, D_SSM)),
            _const_spec((D_SSM, D_SSM)),
            _const_spec((1, D_CONV)),
            _const_spec((1, D_SSM)),
            _const_spec((D_MODEL, D_MODEL)),
        ],
        out_specs=_row_spec(D_MODEL),
        scratch_shapes=[
            pltpu.VMEM((HALO, D_CONV), f32),
            pltpu.VMEM((BATCH, STATE_HALF), f32),
            pltpu.VMEM((BATCH, STATE_HALF), f32),
            pltpu.VMEM((ROWS, STATE_COLS), f32),
            pltpu.VMEM((ROWS, STATE_COLS), bf16),
        ],
        compiler_params=params,
        name="mixer",
    )(h0, row(norm_mix_g[0]), w_in[0].astype(bf16), conv_w[0].astype(f32), wb,
      row(a_re), row(a_im), wc, row(ssm_d[0]), ssm_w_glu[0].astype(bf16),
      row(gain_conv_out[0]), row(gain_ssm_out[0]), w_out[0].astype(bf16))

    y = pl.pallas_call(
        _ffn_kernel,
        out_shape=jax.ShapeDtypeStruct((n_rows, D_MODEL), f32),
        grid=(N_BLOCKS,),
        in_specs=[
            _row_spec(D_MODEL),
            _const_spec((1, D_MODEL)),
            _const_spec((D_MODEL, 2 * D_FF)),
            _const_spec((3, 2 * D_FF)),
            _const_spec((1, 2 * D_FF)),
            _const_spec((D_FF, D_MODEL)),
            _const_spec((1, D_MODEL)),
        ],
        out_specs=_row_spec(D_MODEL),
        scratch_shapes=[pltpu.VMEM((HALO, 2 * D_FF), f32)],
        compiler_params=params,
        name="ffn",
    )(h1, row(norm_ffn_g[0]), w_up[0].astype(bf16), ffn_conv_w[0].astype(f32),
      row(ffn_conv_b[0]), w_down[0].astype(bf16), row(norm_final_g))

    y = y.reshape(SEQ_TOTAL, BATCH, D_MODEL)[N_META:]
    return jnp.swapaxes(y, 0, 1)
```

```python
import jax
import jax.numpy as jnp
from jax.experimental import pallas as pl
from jax.experimental.pallas import tpu as pltpu

D_MODEL = 1024
BATCH = 8
SEQ = 2048
N_META = 16
D_CONV = 512
D_SSM = 512
SSM_GROUP = 16
N_SSM_GROUPS = 32
SSM_STATE = 64
D_FF = 2816
D_IN_PROJ = 3 * D_CONV + D_SSM
RMS_EPS = 1e-6

LANES = 128
MXU_TILE = 256
STATE_COLS = 2 * N_SSM_GROUPS * SSM_STATE
STATE_HALF = STATE_COLS // 2
SCAN_TILES = 4

STEPS_PER_BLOCK = 64
ROWS = STEPS_PER_BLOCK * BATCH
META_ROWS = N_META * BATCH
N_BLOCKS = SEQ // STEPS_PER_BLOCK
HALO = 2 * BATCH
FF_CHUNK = 256
N_FF_CHUNKS = D_FF // FF_CHUNK
VMEM_LIMIT_BYTES = 56 * 1024 * 1024

assert SEQ % STEPS_PER_BLOCK == 0 and STEPS_PER_BLOCK % 2 == 0 and N_META % 2 == 0
assert META_ROWS <= ROWS and D_FF % FF_CHUNK == 0


def _rms_norm(x, gain):
    ms = jnp.mean(x * x, axis=-1, keepdims=True)
    return x * jax.lax.rsqrt(ms + RMS_EPS) * gain


def _dot(a, b):
    return jnp.dot(a, b, preferred_element_type=jnp.float32)


def _causal_conv3(prev, cur, w_ref, rows):
    ext = jnp.concatenate([prev, cur], axis=0)
    return (w_ref[0:1, :] * ext[0:rows]
            + w_ref[1:2, :] * ext[BATCH:rows + BATCH]
            + w_ref[2:3, :] * ext[HALO:rows + HALO])


def _s5_scan(rows, a_re_ref, a_im_ref, s_re_carry, s_im_carry, bu_scr, s_scr):
    for blk in range(STATE_COLS // (2 * LANES * SCAN_TILES)):
        tiles = [blk * SCAN_TILES + k for k in range(SCAN_TILES)]
        a_re = [jnp.broadcast_to(a_re_ref[:, q * LANES:(q + 1) * LANES], (BATCH, LANES))
                for q in tiles]
        a_im = [jnp.broadcast_to(a_im_ref[:, q * LANES:(q + 1) * LANES], (BATCH, LANES))
                for q in tiles]
        col0 = tiles[0] * 2 * LANES
        width = SCAN_TILES * 2 * LANES

        def two_steps(j, carry, a_re=a_re, a_im=a_im, col0=col0, width=width):
            s_re, s_im = carry
            row = pl.multiple_of(j * HALO, HALO)
            bu = bu_scr[pl.ds(row, HALO), col0:col0 + width]
            outs = []
            for step in range(2):
                new_re, new_im, pieces = [], [], []
                for k in range(SCAN_TILES):
                    bu_re = bu[step * BATCH:(step + 1) * BATCH, 2 * k * LANES:(2 * k + 1) * LANES]
                    bu_im = bu[step * BATCH:(step + 1) * BATCH,
                               (2 * k + 1) * LANES:(2 * k + 2) * LANES]
                    n_re = a_re[k] * s_re[k] - a_im[k] * s_im[k] + bu_re
                    n_im = a_re[k] * s_im[k] + a_im[k] * s_re[k] + bu_im
                    new_re.append(n_re)
                    new_im.append(n_im)
                    pieces += [n_re, n_im]
                s_re, s_im = new_re, new_im
                outs.append(jnp.concatenate(pieces, axis=1))
            s_scr[pl.ds(row, HALO), col0:col0 + width] = (
                jnp.concatenate(outs, axis=0).astype(jnp.bfloat16))
            return s_re, s_im

        init = ([s_re_carry[:, q * LANES:(q + 1) * LANES] for q in tiles],
                [s_im_carry[:, q * LANES:(q + 1) * LANES] for q in tiles])
        s_re, s_im = jax.lax.fori_loop(0, rows // HALO, two_steps, init)
        for k, q in enumerate(tiles):
            s_re_carry[:, q * LANES:(q + 1) * LANES] = s_re[k]
            s_im_carry[:, q * LANES:(q + 1) * LANES] = s_im[k]


def _mixer_rows(h, rows, g_mix_ref, w_in_ref, conv_w_ref, wb_ref, a_re_ref, a_im_ref,
                wc_ref, d_ref, w_glu_ref, g_conv_ref, g_ssm_ref, w_out_ref,
                cv_carry, s_re_carry, s_im_carry, bu_scr, s_scr):
    hn = _rms_norm(h, g_mix_ref[...]).astype(jnp.bfloat16)
    proj = _dot(hn, w_in_ref[...])
    b_gate = proj[:, :D_CONV]
    cv = proj[:, D_CONV:2 * D_CONV] * proj[:, 2 * D_CONV:3 * D_CONV]
    u = proj[:, 3 * D_CONV:]

    conv_out = b_gate * _causal_conv3(cv_carry[...], cv, conv_w_ref, rows)
    cv_carry[...] = cv[rows - HALO:, :]

    u_bf = u.astype(jnp.bfloat16)
    for half in range(2):
        bu_scr[0:rows, half * STATE_HALF:(half + 1) * STATE_HALF] = _dot(
            u_bf[:, half * MXU_TILE:(half + 1) * MXU_TILE], wb_ref[half])

    _s5_scan(rows, a_re_ref, a_im_ref, s_re_carry, s_im_carry, bu_scr, s_scr)

    y = jnp.concatenate(
        [_dot(s_scr[0:rows, half * STATE_HALF:(half + 1) * STATE_HALF], wc_ref[half])
         for half in range(2)], axis=1) + d_ref[...] * u
    g = jax.nn.gelu(y, approximate=True)
    ssm_out = g * jax.nn.sigmoid(_dot(g.astype(jnp.bfloat16), w_glu_ref[...]))

    mixed = jnp.concatenate(
        [_rms_norm(conv_out, g_conv_ref[...]), _rms_norm(ssm_out, g_ssm_ref[...])],
        axis=1).astype(jnp.bfloat16)
    return h + _dot(mixed, w_out_ref[...])


def _mixer_kernel(x_ref, meta_ref, *refs):
    param_refs, out_ref, scratch = refs[:12], refs[12], refs[13:]
    cv_carry, s_re_carry, s_im_carry = scratch[:3]
    step = pl.program_id(0)

    @pl.when(step == 0)
    def _():
        cv_carry[...] = jnp.zeros_like(cv_carry)
        s_re_carry[...] = jnp.zeros_like(s_re_carry)
        s_im_carry[...] = jnp.zeros_like(s_im_carry)
        out_ref[0:META_ROWS, :] = _mixer_rows(meta_ref[...], META_ROWS, *param_refs, *scratch)
        out_ref[META_ROWS:, :] = jnp.zeros((ROWS - META_ROWS, D_MODEL), jnp.float32)

    @pl.when(step > 0)
    def _():
        h = pltpu.einshape("btd->tbd", x_ref[...]).reshape(ROWS, D_MODEL)
        out_ref[...] = _mixer_rows(h, ROWS, *param_refs, *scratch)


def _ffn_rows(h, rows, g_ffn_ref, w_up_ref, cw_ref, cb_ref, w_down_ref, g_final_ref,
              up_carry, act_scr):
    hn = _rms_norm(h, g_ffn_ref[...]).astype(jnp.bfloat16)
    for c in range(N_FF_CHUNKS):
        halves = []
        for base in (0, D_FF):
            lo = base + c * FF_CHUNK
            up = _dot(hn, w_up_ref[:, lo:lo + FF_CHUNK])
            conv = (_causal_conv3(up_carry[:, lo:lo + FF_CHUNK], up,
                                  cw_ref.at[:, lo:lo + FF_CHUNK], rows)
                    + cb_ref[:, lo:lo + FF_CHUNK])
            up_carry[:, lo:lo + FF_CHUNK] = up[rows - HALO:, :]
            halves.append(conv)
        act_scr[0:rows, c * FF_CHUNK:(c + 1) * FF_CHUNK] = (
            jax.nn.silu(halves[0]) * halves[1]).astype(jnp.bfloat16)
    out = h + _dot(act_scr[0:rows, :], w_down_ref[...])
    return _rms_norm(out, g_final_ref[...])


def _ffn_kernel(h_ref, *refs):
    param_refs, out_ref, scratch = refs[:6], refs[6], refs[7:]
    up_carry = scratch[0]
    step = pl.program_id(0)

    @pl.when(step == 0)
    def _():
        up_carry[...] = jnp.zeros_like(up_carry)
        _ffn_rows(h_ref[0:META_ROWS, :], META_ROWS, *param_refs, *scratch)

    @pl.when(step > 0)
    def _():
        y = _ffn_rows(h_ref[...], ROWS, *param_refs, *scratch)
        out_ref[...] = pltpu.einshape(
            "tbd->btd", y.reshape(STEPS_PER_BLOCK, BATCH, D_MODEL))


def _const_spec(shape):
    return pl.BlockSpec(shape, lambda i: (0,) * len(shape), pipeline_mode=pl.Buffered(1))


def _x_block_spec():
    return pl.BlockSpec((BATCH, STEPS_PER_BLOCK, D_MODEL),
                        lambda i: (0, jnp.maximum(i - 1, 0), 0))


def _s5_discretize(lam_re, lam_im, log_dt, b_re, b_im):
    dt = jnp.exp(log_dt)[:, None]
    mag = jnp.exp(lam_re * dt)
    ang = lam_im * dt
    a_re = mag * jnp.cos(ang)
    a_im = mag * jnp.sin(ang)
    den = lam_re * lam_re + lam_im * lam_im
    nr = a_re - 1.0
    f_re = (nr * lam_re + a_im * lam_im) / den
    f_im = (a_im * lam_re - nr * lam_im) / den
    bb_re = f_re[..., None] * b_re - f_im[..., None] * b_im
    bb_im = f_re[..., None] * b_im + f_im[..., None] * b_re
    return a_re, a_im, bb_re, bb_im


def _s5_block_weights(bb_re, bb_im, c_re, c_im):
    groups_per_half = N_SSM_GROUPS // 2
    eye = jnp.eye(groups_per_half, dtype=jnp.float32)
    wb, wc = [], []
    for half in range(2):
        sl = slice(half * groups_per_half, (half + 1) * groups_per_half)
        bb = jnp.stack([bb_re[sl], bb_im[sl]], axis=1)
        blk = jnp.einsum("gxph,gk->ghkxp", bb, eye)
        blk = blk.reshape(groups_per_half, SSM_GROUP, groups_per_half // 2, 2, 2, SSM_STATE)
        blk = blk.transpose(0, 1, 2, 4, 3, 5)
        wb.append(blk.reshape(MXU_TILE, STATE_HALF))
        cc = jnp.stack([c_re[sl], -c_im[sl]], axis=1)
        blk = jnp.einsum("gxhp,gk->kxpgh", cc, eye)
        blk = blk.reshape(groups_per_half // 2, 2, 2, SSM_STATE, groups_per_half, SSM_GROUP)
        blk = blk.transpose(0, 2, 1, 3, 4, 5)
        wc.append(blk.reshape(STATE_HALF, MXU_TILE))
    return jnp.stack(wb).astype(jnp.bfloat16), jnp.stack(wc).astype(jnp.bfloat16)


def kernel(x, meta_tokens, norm_mix_g, w_in, conv_w, ssm_lam_re, ssm_lam_im, ssm_log_dt, ssm_b_re, ssm_b_im, ssm_c_re, ssm_c_im, ssm_d, ssm_w_glu, gain_conv_out, gain_ssm_out, w_out, norm_ffn_g, w_up, ffn_conv_w, ffn_conv_b, w_down, norm_final_g):
    f32, bf16 = jnp.float32, jnp.bfloat16
    assert x.shape == (BATCH, SEQ, D_MODEL) and norm_mix_g.shape[0] == 1

    meta_rows = jnp.broadcast_to(meta_tokens.astype(f32)[:, None, :],
                                 (N_META, BATCH, D_MODEL)).reshape(META_ROWS, D_MODEL)
    a_re, a_im, bb_re, bb_im = _s5_discretize(
        ssm_lam_re[0], ssm_lam_im[0], ssm_log_dt[0], ssm_b_re[0], ssm_b_im[0])
    wb, wc = _s5_block_weights(bb_re, bb_im, ssm_c_re[0], ssm_c_im[0])
    row = lambda v: v.reshape(1, -1).astype(f32)

    params = pltpu.CompilerParams(dimension_semantics=("arbitrary",),
                                  vmem_limit_bytes=VMEM_LIMIT_BYTES)
    grid = (N_BLOCKS + 1,)
    h_rows_spec = pl.BlockSpec((ROWS, D_MODEL), lambda i: (i, 0))

    h1 = pl.pallas_call(
        _mixer_kernel,
        out_shape=jax.ShapeDtypeStruct(((N_BLOCKS + 1) * ROWS, D_MODEL), f32),
        grid=grid,
        in_specs=[
            _x_block_spec(),
            _const_spec((META_ROWS, D_MODEL)),
            _const_spec((1, D_MODEL)),
            _const_spec((D_MODEL, D_IN_PROJ)),
            _const_spec((3, D_CONV)),
            _const_spec((2, MXU_TILE, STATE_HALF)),
            _const_spec((1, STATE_HALF)),
            _const_spec((1, STATE_HALF)),
            _const_spec((2, STATE_HALF, MXU_TILE)),
            _const_spec((1, D_SSM)),
            _const_spec((D_SSM, D_SSM)),
            _const_spec((1, D_CONV)),
            _const_spec((1, D_SSM)),
            _const_spec((D_MODEL, D_MODEL)),
        ],
        out_specs=h_rows_spec,
        scratch_shapes=[
            pltpu.VMEM((HALO, D_CONV), f32),
            pltpu.VMEM((BATCH, STATE_HALF), f32),
            pltpu.VMEM((BATCH, STATE_HALF), f32),
            pltpu.VMEM((ROWS, STATE_COLS), f32),
            pltpu.VMEM((ROWS, STATE_COLS), bf16),
        ],
        compiler_params=params,
        name="mixer",
    )(x, meta_rows, row(norm_mix_g[0]), w_in[0].astype(bf16), conv_w[0].astype(f32), wb,
      row(a_re), row(a_im), wc, row(ssm_d[0]), ssm_w_glu[0].astype(bf16),
      row(gain_conv_out[0]), row(gain_ssm_out[0]), w_out[0].astype(bf16))

    return pl.pallas_call(
        _ffn_kernel,
        out_shape=jax.ShapeDtypeStruct((BATCH, SEQ, D_MODEL), f32),
        grid=grid,
        in_specs=[
            h_rows_spec,
            _const_spec((1, D_MODEL)),
            _const_spec((D_MODEL, 2 * D_FF)),
            _const_spec((3, 2 * D_FF)),
            _const_spec((1, 2 * D_FF)),
            _const_spec((D_FF, D_MODEL)),
            _const_spec((1, D_MODEL)),
        ],
        out_specs=_x_block_spec(),
        scratch_shapes=[
            pltpu.VMEM((HALO, 2 * D_FF), f32),
            pltpu.VMEM((ROWS, D_FF), bf16),
        ],
        compiler_params=params,
        name="ffn",
    )(h1, row(norm_ffn_g[0]), w_up[0].astype(bf16), ffn_conv_w[0].astype(f32),
      row(ffn_conv_b[0]), w_down[0].astype(bf16), row(norm_final_g))
```
